```python
import math
import jax
import jax.numpy as jnp
from jax import lax
import numpy as np

D_MODEL = 1024
BATCH = 2
SEQ = 8192
DEPTH = 4
DEC_BATCH = 128
DEC_SEQ = 8
PAST_LEN = 8192
PAGE_SIZE = 128

N_MIXERS = 2
N_HEADS = 16
HEAD_DIM = D_MODEL // N_HEADS
N_KV_HEADS = 4
GROUP = N_HEADS // N_KV_HEADS
QKV_WIDTH = (N_HEADS + 2 * N_KV_HEADS) * HEAD_DIM
WINDOW = 128
BLOCK = 128
NUM_BUCKETS = 32
MAX_DISTANCE = 128
D_FF = 2816
N_SWA = (DEPTH + N_MIXERS - 1) // N_MIXERS
N_SB = DEPTH // N_MIXERS
DN_ALPHA = (2.0 * DEPTH) ** 0.25
DN_BETA = (8.0 * DEPTH) ** -0.25
LN_EPS = 1e-5
SCALE = HEAD_DIM ** -0.5
SB_BIAS_INIT = -6.0

kernel_name = 'hybrid_swa_sink_stickbreaking_macaron_deepnorm_step'


def layer_norm(x, g, b):
    xf = x.astype(jnp.float32)
    mu = jnp.mean(xf, axis=-1, keepdims=True)
    var = jnp.mean(jnp.square(xf - mu), axis=-1, keepdims=True)
    y = (xf - mu) * lax.rsqrt(var + LN_EPS) * g.astype(jnp.float32) + b.astype(jnp.float32)
    return y.astype(x.dtype)


def swiglu(x, w_gate, w_up, w_down):
    return (jax.nn.silu(x @ w_gate) * (x @ w_up)) @ w_down


def split_qkv(h, w_qkv):
    qkv = h @ w_qkv
    lead = qkv.shape[:-1]
    nq = N_HEADS * HEAD_DIM
    nk = N_KV_HEADS * HEAD_DIM
    q = qkv[..., :nq].reshape(*lead, N_KV_HEADS, GROUP, HEAD_DIM)
    k = qkv[..., nq:nq + nk].reshape(*lead, N_KV_HEADS, HEAD_DIM)
    v = qkv[..., nq + nk:].reshape(*lead, N_KV_HEADS, HEAD_DIM)
    return q, k, v


def rel_bucket(dist):
    n = jnp.clip(dist, 0, MAX_DISTANCE)
    max_exact = NUM_BUCKETS // 2
    nf = jnp.maximum(n, 1).astype(jnp.float32)
    large = max_exact + (jnp.log(nf / max_exact) / math.log(MAX_DISTANCE / max_exact)
                         * (NUM_BUCKETS - max_exact)).astype(jnp.int32)
    large = jnp.minimum(large, NUM_BUCKETS - 1)
    return jnp.where(n < max_exact, n, large)


def rel_bias_for(dist, rel_bias):
    b = rel_bias.astype(jnp.float32)[rel_bucket(dist)]
    return jnp.transpose(b, (2, 0, 1)).reshape(N_KV_HEADS, GROUP, *dist.shape)


def sink_softmax(logits, sink):
    s = sink.astype(jnp.float32)[:, :, None, None]
    m = jnp.maximum(jnp.max(logits, axis=-1, keepdims=True), s)
    e = jnp.exp(logits - m)
    return e / (jnp.sum(e, axis=-1, keepdims=True) + jnp.exp(s - m))


def swa_prompt(q, k, v, sink, rel_bias):
    B, S = k.shape[:2]
    nb = S // BLOCK
    qb = q.reshape(B, nb, BLOCK, N_KV_HEADS, GROUP, HEAD_DIM)

    def with_prev(t):
        t = t.reshape(B, nb, BLOCK, N_KV_HEADS, HEAD_DIM)
        prev = jnp.concatenate([jnp.zeros_like(t[:, :1]), t[:, :-1]], axis=1)
        return jnp.concatenate([prev, t], axis=2)

    kk, vv = with_prev(k), with_prev(v)
    scores = jnp.einsum('bnqkgd,bnskd->bnkgqs', qb, kk,
                        preferred_element_type=jnp.float32) * SCALE
    dist = jnp.arange(BLOCK)[:, None] + BLOCK - jnp.arange(2 * BLOCK)[None, :]
    band = (dist >= 0) & (dist <= WINDOW)
    has_prev = (jnp.arange(nb)[:, None, None] > 0) | (jnp.arange(2 * BLOCK)[None, None, :] >= BLOCK)
    valid = band[None] & has_prev
    logits = jnp.where(valid[None, :, None, None], scores + rel_bias_for(dist, rel_bias), -jnp.inf)
    p = sink_softmax(logits, sink)
    out = jnp.einsum('bnkgqs,bnskd->bnqkgd', p.astype(v.dtype), vv)
    return out.reshape(B, S, N_HEADS * HEAD_DIM)


def swa_sample(q, k, v, buf_k, buf_v, sink, rel_bias):
    T = k.shape[1]
    L = buf_k.shape[1]
    kk = jnp.concatenate([buf_k, k], axis=1)
    vv = jnp.concatenate([buf_v, v], axis=1)
    scores = jnp.einsum('btkgd,bskd->bkgts', q, kk,
                        preferred_element_type=jnp.float32) * SCALE
    dist = jnp.arange(T)[:, None] + L - jnp.arange(L + T)[None, :]
    valid = (dist >= 0) & (dist <= WINDOW)
    logits = jnp.where(valid, scores + rel_bias_for(dist, rel_bias), -jnp.inf)
    p = sink_softmax(logits, sink)
    out = jnp.einsum('bkgts,bskd->btkgd', p.astype(v.dtype), vv)
    return out.reshape(q.shape[0], T, N_HEADS * HEAD_DIM), kk[:, T:], vv[:, T:]


def stick_breaking(z, valid):
    log_beta = jax.nn.log_sigmoid(z)
    log_keep = jnp.where(valid, jax.nn.log_sigmoid(-z), 0.0)
    nxt = jnp.concatenate([log_keep[..., 1:], jnp.zeros_like(log_keep[..., :1])], axis=-1)
    log_surv = lax.cumsum(nxt, axis=z.ndim - 1, reverse=True)
    return jnp.where(valid, jnp.exp(log_beta + log_surv), 0.0)


def sb_prompt(q, k, v, hbias):
    B, S = k.shape[:2]
    nb = S // BLOCK
    qb = jnp.moveaxis(q.reshape(B, nb, BLOCK, N_KV_HEADS, GROUP, HEAD_DIM), 1, 0)
    s_pos = jnp.arange(S)

    def one_block(args):
        q_blk, n = args
        t_pos = n * BLOCK + jnp.arange(BLOCK)
        z = jnp.einsum('bqkgd,bskd->bkgqs', q_blk, k,
                       preferred_element_type=jnp.float32) * SCALE + hbias
        w = stick_breaking(z, s_pos[None, :] < t_pos[:, None])
        return jnp.einsum('bkgqs,bskd->bqkgd', w.astype(v.dtype), v)

    out = lax.map(one_block, (qb, jnp.arange(nb)))
    return jnp.moveaxis(out, 0, 1).reshape(B, S, N_HEADS * HEAD_DIM)


def sb_sample(q, k, v, pool_k, pool_v, page_table, hbias):
    Bd, T = k.shape[:2]
    n_past = page_table.shape[1] * PAGE_SIZE
    t_pos = n_past + jnp.arange(T)
    s_pos = jnp.arange(n_past + T)
    valid = s_pos[None, :] < t_pos[:, None]

    def one_seq(args):
        q_b, k_b, v_b, pages = args
        kk = jnp.concatenate([pool_k[pages].reshape(n_past, N_KV_HEADS, HEAD_DIM), k_b], axis=0)
        vv = jnp.concatenate([pool_v[pages].reshape(n_past, N_KV_HEADS, HEAD_DIM), v_b], axis=0)
        z = jnp.einsum('tkgd,skd->kgts', q_b, kk,
                       preferred_element_type=jnp.float32) * SCALE + hbias
        w = stick_breaking(z, valid)
        return jnp.einsum('kgts,skd->tkgd', w.astype(vv.dtype), vv)

    out = lax.map(one_seq, (q, k, v, page_table))
    return out.reshape(Bd, T, N_HEADS * HEAD_DIM)


def setup_inputs(seed: int = 0) -> dict:
    key = jax.random.key(seed)
    ks = jax.random.split(key, 20)
    n_pages = PAST_LEN // PAGE_SIZE
    n_phys = (DEC_BATCH * n_pages * 5) // 4
    page_table = jax.random.permutation(ks[0], n_phys)[:DEC_BATCH * n_pages]
    page_table = page_table.reshape(DEC_BATCH, n_pages).astype(jnp.int32)

    def nrm(k, shape, s=1.0):
        return s * jax.random.normal(k, shape, jnp.float32)

    hd = N_HEADS * HEAD_DIM
    return {
        'x_prompt': nrm(ks[1], (BATCH, SEQ, D_MODEL)),
        'x_sample': nrm(ks[2], (DEC_BATCH, DEC_SEQ, D_MODEL)),
        'cache_swa_k': nrm(ks[3], (N_SWA, DEC_BATCH, WINDOW, N_KV_HEADS, HEAD_DIM)),
        'cache_swa_v': nrm(ks[4], (N_SWA, DEC_BATCH, WINDOW, N_KV_HEADS, HEAD_DIM)),
        'cache_sb_k': nrm(ks[5], (N_SB, n_phys, PAGE_SIZE, N_KV_HEADS, HEAD_DIM)),
        'cache_sb_v': nrm(ks[6], (N_SB, n_phys, PAGE_SIZE, N_KV_HEADS, HEAD_DIM)),
        'page_table': page_table,
        'rel_bias': nrm(ks[7], (NUM_BUCKETS, N_HEADS), 0.5),
        'attn_sinks': nrm(ks[8], (N_SWA, N_HEADS), 0.5),
        'sb_logit_bias': SB_BIAS_INIT + nrm(ks[19], (N_SB, N_HEADS), 0.3),
        'w_qkv': nrm(ks[9], (DEPTH, D_MODEL, QKV_WIDTH), D_MODEL ** -0.5),
        'w_o': nrm(ks[10], (DEPTH, hd, D_MODEL), DN_BETA * hd ** -0.5),
        'w_ffn1_gate': nrm(ks[11], (DEPTH, D_MODEL, D_FF), D_MODEL ** -0.5),
        'w_ffn1_up': nrm(ks[12], (DEPTH, D_MODEL, D_FF), D_MODEL ** -0.5),
        'w_ffn1_down': nrm(ks[13], (DEPTH, D_FF, D_MODEL), DN_BETA * D_FF ** -0.5),
        'w_ffn2_gate': nrm(ks[14], (DEPTH, D_MODEL, D_FF), D_MODEL ** -0.5),
        'w_ffn2_up': nrm(ks[15], (DEPTH, D_MODEL, D_FF), D_MODEL ** -0.5),
        'w_ffn2_down': nrm(ks[16], (DEPTH, D_FF, D_MODEL), DN_BETA * D_FF ** -0.5),
        'ln_gain': 1.0 + nrm(ks[17], (DEPTH, 3, D_MODEL), 0.02),
        'ln_bias': nrm(ks[18], (DEPTH, 3, D_MODEL), 0.02),
    }


def reference(x_prompt, x_sample, cache_swa_k, cache_swa_v, cache_sb_k, cache_sb_v,
              page_table, rel_bias, attn_sinks, sb_logit_bias, w_qkv, w_o,
              w_ffn1_gate, w_ffn1_up, w_ffn1_down, w_ffn2_gate, w_ffn2_up, w_ffn2_down,
              ln_gain, ln_bias):
    xp, xs = x_prompt, x_sample
    swa_kp, swa_vp, swa_ks, swa_vs = [], [], [], []
    sb_kp, sb_vp, sb_ks, sb_vs = [], [], [], []

    def post(h, res, i, j):
        return layer_norm(DN_ALPHA * h + res, ln_gain[i, j], ln_bias[i, j])

    for i in range(DEPTH):
        xp = post(xp, 0.5 * swiglu(xp, w_ffn1_gate[i], w_ffn1_up[i], w_ffn1_down[i]), i, 0)
        xs = post(xs, 0.5 * swiglu(xs, w_ffn1_gate[i], w_ffn1_up[i], w_ffn1_down[i]), i, 0)
        qp, kp, vp = split_qkv(xp, w_qkv[i])
        qs, ks_, vs_ = split_qkv(xs, w_qkv[i])
        j = i // N_MIXERS
        if i % N_MIXERS == 0:
            sink = attn_sinks[j].reshape(N_KV_HEADS, GROUP)
            mp = swa_prompt(qp, kp, vp, sink, rel_bias)
            ms, nk, nv = swa_sample(qs, ks_, vs_, cache_swa_k[j], cache_swa_v[j], sink, rel_bias)
            swa_kp.append(kp[:, -WINDOW:])
            swa_vp.append(vp[:, -WINDOW:])
            swa_ks.append(nk)
            swa_vs.append(nv)
        else:
            hbias = sb_logit_bias[j].astype(jnp.float32).reshape(N_KV_HEADS, GROUP)[:, :, None, None]
            mp = sb_prompt(qp, kp, vp, hbias)
            ms = sb_sample(qs, ks_, vs_, cache_sb_k[j], cache_sb_v[j], page_table, hbias)
            sb_kp.append(kp)
            sb_vp.append(vp)
            sb_ks.append(ks_)
            sb_vs.append(vs_)
        xp = post(xp, mp @ w_o[i], i, 1)
        xs = post(xs, ms @ w_o[i], i, 1)
        xp = post(xp, 0.5 * swiglu(xp, w_ffn2_gate[i], w_ffn2_up[i], w_ffn2_down[i]), i, 2)
        xs = post(xs, 0.5 * swiglu(xs, w_ffn2_gate[i], w_ffn2_up[i], w_ffn2_down[i]), i, 2)

    return (xp, xs,
            jnp.stack(swa_kp), jnp.stack(swa_vp), jnp.stack(sb_kp), jnp.stack(sb_vp),
            jnp.stack(swa_ks), jnp.stack(swa_vs), jnp.stack(sb_ks), jnp.stack(sb_vs))
```

```python
import functools
import math

import jax
import jax.numpy as jnp
from jax import lax
from jax.experimental import pallas as pl
from jax.experimental.pallas import tpu as pltpu

D_MODEL = 1024
N_HEADS = 16
HEAD_DIM = 64
N_KV_HEADS = 4
GROUP = N_HEADS // N_KV_HEADS
KV_WIDTH = N_KV_HEADS * HEAD_DIM
WINDOW = 128
BLOCK = 128
PAGE_SIZE = 128
NUM_BUCKETS = 32
MAX_DISTANCE = 128
LN_EPS = 1e-5
SCALE = HEAD_DIM ** -0.5
NEG = -1e30

BF16 = jnp.bfloat16
F32 = jnp.float32

VMEM_LIMIT = 56 * 1024 * 1024


def _params(sem):
    return pltpu.CompilerParams(dimension_semantics=sem, vmem_limit_bytes=VMEM_LIMIT)


def _dot(a, b):
    return jnp.dot(a, b, preferred_element_type=F32)


def _dot_nt(a, b):
    return lax.dot_general(a, b, (((1,), (1,)), ((), ())), preferred_element_type=F32)


def _layer_norm(y, g, b):
    mu = jnp.mean(y, axis=-1, keepdims=True)
    d = y - mu
    var = jnp.mean(d * d, axis=-1, keepdims=True)
    return d * lax.rsqrt(var + LN_EPS) * g + b


def _ffn_kernel(x_ref, wg_ref, wu_ref, wd_ref, g_ref, b_ref, o_ref, xb_ref, acc_ref, *, alpha):
    j = pl.program_id(1)

    @pl.when(j == 0)
    def _():
        xb_ref[...] = x_ref[...].astype(BF16)
        acc_ref[...] = jnp.zeros_like(acc_ref)

    xb = xb_ref[...]
    gate = _dot(xb, wg_ref[...])
    up = _dot(xb, wu_ref[...])
    h = gate * jax.nn.sigmoid(gate) * up
    acc_ref[...] += _dot(h.astype(BF16), wd_ref[...])

    @pl.when(j == pl.num_programs(1) - 1)
    def _():
        y = alpha * x_ref[...] + 0.5 * acc_ref[...]
        o_ref[...] = _layer_norm(y, g_ref[...], b_ref[...])


def _ffn_ln(x, wg, wu, wd, gain, bias, alpha, tm, tf):
    n, d = x.shape
    dff = wg.shape[1]
    return pl.pallas_call(
        functools.partial(_ffn_kernel, alpha=alpha),
        grid=(n // tm, dff // tf),
        in_specs=[
            pl.BlockSpec((tm, d), lambda i, j: (i, 0)),
            pl.BlockSpec((d, tf), lambda i, j: (0, j)),
            pl.BlockSpec((d, tf), lambda i, j: (0, j)),
            pl.BlockSpec((tf, d), lambda i, j: (j, 0)),
            pl.BlockSpec((1, d), lambda i, j: (0, 0)),
            pl.BlockSpec((1, d), lambda i, j: (0, 0)),
        ],
        out_specs=pl.BlockSpec((tm, d), lambda i, j: (i, 0)),
        out_shape=jax.ShapeDtypeStruct((n, d), F32),
        scratch_shapes=[pltpu.VMEM((tm, d), BF16), pltpu.VMEM((tm, d), F32)],
        compiler_params=_params(("parallel", "arbitrary")),
        name="ffn_ln",
    )(x, wg, wu, wd, gain, bias)


def _qkv_kernel(x_ref, w_ref, q_ref, k_ref, v_ref, kb_ref, vb_ref):
    nq = N_HEADS * HEAD_DIM
    qkv = _dot(x_ref[...].astype(BF16), w_ref[...])
    q_ref[...] = (qkv[:, :nq] * SCALE).astype(BF16)
    k = qkv[:, nq:nq + KV_WIDTH]
    v = qkv[:, nq + KV_WIDTH:]
    k_ref[...] = k
    v_ref[...] = v
    kb_ref[...] = k.astype(BF16)
    vb_ref[...] = v.astype(BF16)


def _qkv(x, w, tm):
    n, d = x.shape
    width = w.shape[1]
    nq = N_HEADS * HEAD_DIM
    row = lambda i: (i, 0)
    return pl.pallas_call(
        _qkv_kernel,
        grid=(n // tm,),
        in_specs=[pl.BlockSpec((tm, d), row), pl.BlockSpec((d, width), lambda i: (0, 0))],
        out_specs=[
            pl.BlockSpec((tm, nq), row),
            pl.BlockSpec((tm, KV_WIDTH), row),
            pl.BlockSpec((tm, KV_WIDTH), row),
            pl.BlockSpec((tm, KV_WIDTH), row),
            pl.BlockSpec((tm, KV_WIDTH), row),
        ],
        out_shape=[
            jax.ShapeDtypeStruct((n, nq), BF16),
            jax.ShapeDtypeStruct((n, KV_WIDTH), F32),
            jax.ShapeDtypeStruct((n, KV_WIDTH), F32),
            jax.ShapeDtypeStruct((n, KV_WIDTH), BF16),
            jax.ShapeDtypeStruct((n, KV_WIDTH), BF16),
        ],
        compiler_params=_params(("parallel",)),
        name="qkv_proj",
    )(x, w)


def _oproj_kernel(mp_ref, ms_ref, x_ref, w_ref, g_ref, b_ref, o_ref, *, alpha, n_prompt_tiles):
    i = pl.program_id(0)

    def finish(m):
        y = alpha * x_ref[...] + _dot(m, w_ref[...])
        o_ref[...] = _layer_norm(y, g_ref[...], b_ref[...])

    @pl.when(i < n_prompt_tiles)
    def _():
        finish(mp_ref[...])

    @pl.when(i >= n_prompt_tiles)
    def _():
        finish(ms_ref[...])


def _oproj_ln(mp, ms, x, w, gain, bias, alpha, tm):
    n, d = x.shape
    npt = mp.shape[0] // tm
    nst = ms.shape[0] // tm
    return pl.pallas_call(
        functools.partial(_oproj_kernel, alpha=alpha, n_prompt_tiles=npt),
        grid=(n // tm,),
        in_specs=[
            pl.BlockSpec((tm, d), lambda i: (jnp.minimum(i, npt - 1), 0)),
            pl.BlockSpec((tm, d), lambda i: (jnp.clip(i - npt, 0, nst - 1), 0)),
            pl.BlockSpec((tm, d), lambda i: (i, 0)),
            pl.BlockSpec((d, d), lambda i: (0, 0)),
            pl.BlockSpec((1, d), lambda i: (0, 0)),
            pl.BlockSpec((1, d), lambda i: (0, 0)),
        ],
        out_specs=pl.BlockSpec((tm, d), lambda i: (i, 0)),
        out_shape=jax.ShapeDtypeStruct((n, d), F32),
        compiler_params=_params(("parallel",)),
        name="oproj_ln",
    )(mp, ms, x, w, gain, bias)


def _swa_prompt_kernel(sink_ref, q_ref, kp_ref, kc_ref, vp_ref, vc_ref, bp_ref, bc_ref, o_ref):
    n = pl.program_id(1)
    m_rows = GROUP * BLOCK
    for kvh in range(N_KV_HEADS):
        cs = slice(kvh * HEAD_DIM, (kvh + 1) * HEAD_DIM)
        q = jnp.concatenate(
            [q_ref[:, (kvh * GROUP + g) * HEAD_DIM:(kvh * GROUP + g + 1) * HEAD_DIM] for g in range(GROUP)],
            axis=0)
        sink = jnp.concatenate(
            [jnp.full((BLOCK, 1), sink_ref[kvh * GROUP + g], F32) for g in range(GROUP)], axis=0)
        sc = _dot_nt(q, kc_ref[:, cs]) + bc_ref[kvh]
        sp = _dot_nt(q, kp_ref[:, cs]) + bp_ref[kvh]
        sp = jnp.where(n > 0, sp, NEG)
        m = jnp.maximum(jnp.maximum(jnp.max(sc, axis=-1, keepdims=True),
                                    jnp.max(sp, axis=-1, keepdims=True)), sink)
        ec = jnp.exp(sc - m)
        ep = jnp.exp(sp - m)
        den = (jnp.sum(ec, axis=-1, keepdims=True) + jnp.sum(ep, axis=-1, keepdims=True)
               + jnp.exp(sink - m))
        o = (_dot(ec.astype(BF16), vc_ref[:, cs]) + _dot(ep.astype(BF16), vp_ref[:, cs])) / den
        for g in range(GROUP):
            h = kvh * GROUP + g
            o_ref[:, h * HEAD_DIM:(h + 1) * HEAD_DIM] = o[g * BLOCK:(g + 1) * BLOCK].astype(BF16)
    del m_rows


def _swa_prompt(q, kb, vb, bias_prev, bias_cur, sinks, batch, seq):
    nb = seq // BLOCK
    nq = N_HEADS * HEAD_DIM
    cur = lambda b, n: (b * nb + n, 0)
    prev = lambda b, n: (b * nb + jnp.maximum(n - 1, 0), 0)
    full3 = lambda b, n: (0, 0, 0)
    return pl.pallas_call(
        _swa_prompt_kernel,
        grid=(batch, nb),
        in_specs=[
            pl.BlockSpec(memory_space=pltpu.SMEM),
            pl.BlockSpec((BLOCK, nq), cur),
            pl.BlockSpec((BLOCK, KV_WIDTH), prev),
            pl.BlockSpec((BLOCK, KV_WIDTH), cur),
            pl.BlockSpec((BLOCK, KV_WIDTH), prev),
            pl.BlockSpec((BLOCK, KV_WIDTH), cur),
            pl.BlockSpec((N_KV_HEADS, GROUP * BLOCK, BLOCK), full3),
            pl.BlockSpec((N_KV_HEADS, GROUP * BLOCK, BLOCK), full3),
        ],
        out_specs=pl.BlockSpec((BLOCK, nq), cur),
        out_shape=jax.ShapeDtypeStruct((batch * seq, nq), BF16),
        compiler_params=_params(("parallel", "arbitrary")),
        name="swa_prompt",
    )(sinks, q, kb, kb, vb, vb, bias_prev, bias_cur)


def _row_kv_mask():
    rows = N_HEADS * 8
    r = lax.broadcasted_iota(jnp.int32, (rows, KV_WIDTH), 0)
    c = lax.broadcasted_iota(jnp.int32, (rows, KV_WIDTH), 1)
    return (r >> 5) == (c >> 6)


def _block_diag_q(q_rows, mask):
    q4 = jnp.concatenate([q_rows] * N_KV_HEADS, axis=-1)
    return jnp.where(mask, q4, jnp.zeros_like(q4))


def _pick_kv_block(acc, mask):
    z = jnp.where(mask, acc, 0.0)
    out = z[:, 0:HEAD_DIM]
    for kvh in range(1, N_KV_HEADS):
        out = out + z[:, kvh * HEAD_DIM:(kvh + 1) * HEAD_DIM]
    return out


def _swa_sample_kernel(sink_ref, q_ref, kt_ref, vt_ref, kn_ref, vn_ref, bb_ref, bn_ref, o_ref, *, seqs):
    mask = _row_kv_mask()
    rows = N_HEADS * 8
    sink = jnp.concatenate([jnp.full((8, 1), sink_ref[h], F32) for h in range(N_HEADS)], axis=0)
    for s in range(seqs):
        qbd = _block_diag_q(q_ref[s], mask)
        sb = _dot(qbd, kt_ref[s].astype(BF16)) + bb_ref[...]
        sn = _dot_nt(qbd, kn_ref[s]) + bn_ref[...]
        m = jnp.maximum(jnp.maximum(jnp.max(sb, axis=-1, keepdims=True),
                                    jnp.max(sn, axis=-1, keepdims=True)), sink)
        eb = jnp.exp(sb - m)
        en = jnp.exp(sn - m)
        den = (jnp.sum(eb, axis=-1, keepdims=True) + jnp.sum(en, axis=-1, keepdims=True)
               + jnp.exp(sink - m))
        acc = _dot_nt(eb.astype(BF16), vt_ref[s].astype(BF16)) + _dot(en.astype(BF16), vn_ref[s])
        o_ref[s] = (_pick_kv_block(acc, mask) / den).astype(BF16)
    del rows


def _swa_sample(q_rows, kt, vt, kn, vn, bias_buf, bias_new, sinks, layer, seqs):
    db = q_rows.shape[0]
    rows = N_HEADS * 8
    seq3 = lambda i: (i, 0, 0)
    cache = lambda i: (layer, i, 0, 0)
    return pl.pallas_call(
        functools.partial(_swa_sample_kernel, seqs=seqs),
        grid=(db // seqs,),
        in_specs=[
            pl.BlockSpec(memory_space=pltpu.SMEM),
            pl.BlockSpec((seqs, rows, HEAD_DIM), seq3),
            pl.BlockSpec((None, seqs, KV_WIDTH, WINDOW), cache),
            pl.BlockSpec((None, seqs, KV_WIDTH, WINDOW), cache),
            pl.BlockSpec((seqs, WINDOW, KV_WIDTH), seq3),
            pl.BlockSpec((seqs, WINDOW, KV_WIDTH), seq3),
            pl.BlockSpec((rows, WINDOW), lambda i: (0, 0)),
            pl.BlockSpec((rows, WINDOW), lambda i: (0, 0)),
        ],
        out_specs=pl.BlockSpec((seqs, rows, HEAD_DIM), seq3),
        out_shape=jax.ShapeDtypeStruct((db, rows, HEAD_DIM), BF16),
        compiler_params=_params(("parallel",)),
        name="swa_sample",
    )(sinks, q_rows, kt, vt, kn, vn, bias_buf, bias_new)


def _sb_weights(z, tri, carry, valid):
    l = jnp.log(1.0 + jnp.exp(-jnp.abs(z)))
    log_keep = -jnp.maximum(z, 0.0) - l
    if valid is not None:
        log_keep = jnp.where(valid, log_keep, 0.0)
    log_beta = jnp.minimum(z, 0.0) - l
    hi = log_keep.astype(BF16)
    lo = (log_keep - hi.astype(F32)).astype(BF16)
    suffix = _dot(hi, tri) + _dot(lo, tri)
    w = jnp.exp(log_beta + suffix + carry)
    if valid is not None:
        w = jnp.where(valid, w, 0.0)
    return w, carry + jnp.sum(log_keep, axis=-1, keepdims=True)


def _sb_prompt_kernel(hb_ref, q_ref, k_ref, v_ref, tri_ref, o_ref, acc_ref, cs_ref, *, tq):
    i = pl.program_id(1)
    m_rows = GROUP * tq
    tri = tri_ref[...]
    t_in = lax.broadcasted_iota(jnp.int32, (m_rows, tq), 0) & (tq - 1)
    s_in = lax.broadcasted_iota(jnp.int32, (m_rows, tq), 1)
    diag_valid = s_in < t_in
    for kvh in range(N_KV_HEADS):
        cs = slice(kvh * HEAD_DIM, (kvh + 1) * HEAD_DIM)
        q = jnp.concatenate(
            [q_ref[:, (kvh * GROUP + g) * HEAD_DIM:(kvh * GROUP + g + 1) * HEAD_DIM] for g in range(GROUP)],
            axis=0)
        hb = jnp.concatenate(
            [jnp.full((tq, 1), hb_ref[kvh * GROUP + g], F32) for g in range(GROUP)], axis=0)

        def block(j, valid):
            rows = pl.ds(pl.multiple_of(j * tq, tq), tq)
            z = _dot_nt(q, k_ref[rows, cs]) + hb
            w, carry = _sb_weights(z, tri, cs_ref[...], valid)
            acc_ref[...] += _dot(w.astype(BF16), v_ref[rows, cs])
            cs_ref[...] = carry

        acc_ref[...] = jnp.zeros_like(acc_ref)
        cs_ref[...] = jnp.zeros_like(cs_ref)
        block(i, diag_valid)

        def body(jj, c):
            block(i - 1 - jj, None)
            return c

        lax.fori_loop(0, i, body, 0)
        out = acc_ref[...]
        for g in range(GROUP):
            h = kvh * GROUP + g
            o_ref[:, h * HEAD_DIM:(h + 1) * HEAD_DIM] = out[g * tq:(g + 1) * tq].astype(BF16)


def _sb_prompt(q, kb, vb, tri, hbias, batch, seq, tq):
    nq_blocks = seq // tq
    nq = N_HEADS * HEAD_DIM
    return pl.pallas_call(
        functools.partial(_sb_prompt_kernel, tq=tq),
        grid=(batch, nq_blocks),
        in_specs=[
            pl.BlockSpec(memory_space=pltpu.SMEM),
            pl.BlockSpec((tq, nq), lambda b, i: (b * nq_blocks + i, 0)),
            pl.BlockSpec((seq, KV_WIDTH), lambda b, i: (b, 0)),
            pl.BlockSpec((seq, KV_WIDTH), lambda b, i: (b, 0)),
            pl.BlockSpec((tq, tq), lambda b, i: (0, 0)),
        ],
        out_specs=pl.BlockSpec((tq, nq), lambda b, i: (b * nq_blocks + i, 0)),
        out_shape=jax.ShapeDtypeStruct((batch * seq, nq), BF16),
        scratch_shapes=[pltpu.VMEM((GROUP * tq, HEAD_DIM), F32), pltpu.VMEM((GROUP * tq, 1), F32)],
        compiler_params=_params(("parallel", "arbitrary")),
        name="sb_prompt",
    )(hbias, q, kb, vb, tri)


def _sb_sample_kernel(pt_ref, hb_ref, q_ref, kn_ref, vn_ref, tri_ref, *rest, pages):
    del pt_ref
    k_refs = rest[:pages]
    v_refs = rest[pages:2 * pages]
    o_ref, acc_ref, cs_ref, qbd_ref = rest[2 * pages:]
    c = pl.program_id(1)
    rows = N_HEADS * 8
    mask = _row_kv_mask()
    tri = tri_ref[...]
    hb = jnp.concatenate([jnp.full((8, 1), hb_ref[h], F32) for h in range(N_HEADS)], axis=0)

    def page(kt, vt, valid):
        z = _dot(qbd_ref[...], kt) + hb
        w, carry = _sb_weights(z, tri, cs_ref[...], valid)
        acc_ref[...] += _dot_nt(w.astype(BF16), vt)
        cs_ref[...] = carry

    @pl.when(c == 0)
    def _():
        qbd_ref[...] = _block_diag_q(q_ref[...], mask)
        acc_ref[...] = jnp.zeros_like(acc_ref)
        cs_ref[...] = jnp.zeros_like(cs_ref)
        t_in = lax.broadcasted_iota(jnp.int32, (rows, PAGE_SIZE), 0) & 7
        s_in = lax.broadcasted_iota(jnp.int32, (rows, PAGE_SIZE), 1)
        page(kn_ref[...], vn_ref[...], s_in < t_in)

    for p in range(pages):
        page(k_refs[p][...].astype(BF16), v_refs[p][...].astype(BF16), None)

    @pl.when(c == pl.num_programs(1) - 1)
    def _():
        o_ref[...] = _pick_kv_block(acc_ref[...], mask).astype(BF16)


def _sb_sample(q_rows, knt, vnt, pool_kt, pool_vt, page_table, tri, hbias, layer, pages):
    db, n_pages = page_table.shape
    rows = N_HEADS * 8
    n_chunks = n_pages // pages

    def page_map(p):
        return lambda b, c, pt: (layer, pt[b, n_pages - 1 - (c * pages + p)], 0, 0)

    seq3 = lambda b, c, pt: (b, 0, 0)
    page_spec = [pl.BlockSpec((None, None, KV_WIDTH, PAGE_SIZE), page_map(p)) for p in range(pages)]
    grid_spec = pltpu.PrefetchScalarGridSpec(
        num_scalar_prefetch=1,
        grid=(db, n_chunks),
        in_specs=[
            pl.BlockSpec(memory_space=pltpu.SMEM),
            pl.BlockSpec((None, rows, HEAD_DIM), seq3),
            pl.BlockSpec((None, KV_WIDTH, PAGE_SIZE), seq3),
            pl.BlockSpec((None, KV_WIDTH, PAGE_SIZE), seq3),
            pl.BlockSpec((PAGE_SIZE, PAGE_SIZE), lambda b, c, pt: (0, 0)),
        ] + page_spec + page_spec,
        out_specs=pl.BlockSpec((None, rows, HEAD_DIM), seq3),
        scratch_shapes=[
            pltpu.VMEM((rows, KV_WIDTH), F32),
            pltpu.VMEM((rows, 1), F32),
            pltpu.VMEM((rows, KV_WIDTH), BF16),
        ],
    )
    return pl.pallas_call(
        functools.partial(_sb_sample_kernel, pages=pages),
        grid_spec=grid_spec,
        out_shape=jax.ShapeDtypeStruct((db, rows, HEAD_DIM), BF16),
        compiler_params=_params(("parallel", "arbitrary")),
        name="sb_sample",
    )(page_table, hbias, q_rows, knt, vnt, tri, *([pool_kt] * pages), *([pool_vt] * pages))


def _rel_bucket(dist):
    n = jnp.clip(dist, 0, MAX_DISTANCE)
    max_exact = NUM_BUCKETS // 2
    nf = jnp.maximum(n, 1).astype(F32)
    large = max_exact + (jnp.log(nf / max_exact) / math.log(MAX_DISTANCE / max_exact)
                         * (NUM_BUCKETS - max_exact)).astype(jnp.int32)
    large = jnp.minimum(large, NUM_BUCKETS - 1)
    return jnp.where(n < max_exact, n, large)


def _masked_bias(dist, rel_bias):
    b = rel_bias.astype(F32)[_rel_bucket(dist)]
    b = jnp.transpose(b, (2, 0, 1))
    return jnp.where(((dist >= 0) & (dist <= WINDOW))[None], b, NEG)


def _tri(n):
    j = lax.broadcasted_iota(jnp.int32, (n, n), 0)
    s = lax.broadcasted_iota(jnp.int32, (n, n), 1)
    return (j > s).astype(BF16)


def _tables(rel_bias, tokens, tq):
    q_i = jnp.arange(BLOCK)[:, None]
    s_i = jnp.arange(BLOCK)[None, :]
    kv_rows = lambda b: b.reshape(N_KV_HEADS, GROUP * b.shape[1], b.shape[2])
    t_i = jnp.arange(tokens)[:, None]
    w_i = jnp.arange(WINDOW)[None, :]
    return {
        "bias_prev": kv_rows(_masked_bias(q_i + BLOCK - s_i, rel_bias)),
        "bias_cur": kv_rows(_masked_bias(q_i - s_i, rel_bias)),
        "bias_buf": _masked_bias(t_i + WINDOW - w_i, rel_bias).reshape(N_HEADS * tokens, WINDOW),
        "bias_new": jnp.where(w_i < tokens, _masked_bias(t_i - w_i, rel_bias), NEG).reshape(
            N_HEADS * tokens, WINDOW),
        "tri_q": _tri(tq),
        "tri_p": _tri(PAGE_SIZE),
    }


def _to_rows(t, db, tokens, width_heads):
    t = t.reshape(db, tokens, width_heads, HEAD_DIM)
    return jnp.transpose(t, (0, 2, 1, 3)).reshape(db, width_heads * tokens, HEAD_DIM)


def _from_rows(t, db, tokens):
    t = t.reshape(db, N_HEADS, tokens, HEAD_DIM)
    return jnp.transpose(t, (0, 2, 1, 3)).reshape(db * tokens, N_HEADS * HEAD_DIM)


def _new_keys_t(t, db, tokens):
    t = jnp.transpose(t.reshape(db, tokens, KV_WIDTH), (0, 2, 1)).astype(BF16)
    return jnp.pad(t, ((0, 0), (0, 0), (0, PAGE_SIZE - tokens)))


def _new_keys(t, db, tokens):
    t = t.reshape(db, tokens, KV_WIDTH).astype(BF16)
    return jnp.pad(t, ((0, 0), (0, WINDOW - tokens), (0, 0)))


def _cache_t(cache):
    l, x, r = cache.shape[:3]
    return jnp.transpose(cache, (0, 1, 3, 4, 2)).reshape(l, x, KV_WIDTH, r)


def kernel(x_prompt, x_sample, cache_swa_k, cache_swa_v, cache_sb_k, cache_sb_v, page_table, rel_bias,
           attn_sinks, sb_logit_bias, w_qkv, w_o, w_ffn1_gate, w_ffn1_up, w_ffn1_down, w_ffn2_gate,
           w_ffn2_up, w_ffn2_down, ln_gain, ln_bias):
    batch, seq, d = x_prompt.shape
    db, tokens, _ = x_sample.shape
    depth = w_qkv.shape[0]
    dff = w_ffn1_gate.shape[2]
    n_p = batch * seq
    n_s = db * tokens
    assert tokens == 8 and d == D_MODEL
    alpha = (2.0 * depth) ** 0.25

    tm = math.gcd(1024, math.gcd(n_p, n_s))
    tf = 256 if dff % 256 == 0 else 128
    tq = min(256, seq)
    pages = math.gcd(8, page_table.shape[1])
    seqs = math.gcd(8, db)

    x = jnp.concatenate([x_prompt.reshape(n_p, d), x_sample.reshape(n_s, d)], axis=0)

    tabs = _tables(rel_bias, tokens, tq)
    bias_prev, bias_cur, bias_buf, bias_new = (tabs[n] for n in ("bias_prev", "bias_cur", "bias_buf", "bias_new"))
    tri_q, tri_p = tabs["tri_q"], tabs["tri_p"]

    swa_kt, swa_vt = _cache_t(cache_swa_k), _cache_t(cache_swa_v)
    sb_kt, sb_vt = _cache_t(cache_sb_k), _cache_t(cache_sb_v)

    bf = lambda w: w.astype(BF16)
    row = lambda v: v.reshape(1, d)

    swa_kp, swa_vp, swa_ks, swa_vs = [], [], [], []
    sb_kp, sb_vp, sb_ks, sb_vs = [], [], [], []
    for i in range(depth):
        x = _ffn_ln(x, bf(w_ffn1_gate[i]), bf(w_ffn1_up[i]), bf(w_ffn1_down[i]),
                    row(ln_gain[i, 0]), row(ln_bias[i, 0]), alpha, tm, tf)
        q, k, v, kb, vb = _qkv(x, bf(w_qkv[i]), tm)
        kp = k[:n_p].reshape(batch, seq, N_KV_HEADS, HEAD_DIM)
        vp = v[:n_p].reshape(batch, seq, N_KV_HEADS, HEAD_DIM)
        ks = k[n_p:].reshape(db, tokens, N_KV_HEADS, HEAD_DIM)
        vs = v[n_p:].reshape(db, tokens, N_KV_HEADS, HEAD_DIM)
        q_rows = _to_rows(q[n_p:], db, tokens, N_HEADS)
        j = i // 2
        if i % 2 == 0:
            sinks = attn_sinks[j].astype(F32)
            mp = _swa_prompt(q, kb, vb, bias_prev, bias_cur, sinks, batch, seq)
            ms = _swa_sample(q_rows, swa_kt, swa_vt, _new_keys(k[n_p:], db, tokens),
                             _new_keys(v[n_p:], db, tokens), bias_buf, bias_new, sinks, j, seqs)
            swa_kp.append(kp[:, -WINDOW:])
            swa_vp.append(vp[:, -WINDOW:])
            swa_ks.append(jnp.concatenate([cache_swa_k[j], ks], axis=1)[:, tokens:])
            swa_vs.append(jnp.concatenate([cache_swa_v[j], vs], axis=1)[:, tokens:])
        else:
            hbias = sb_logit_bias[j].astype(F32)
            mp = _sb_prompt(q, kb, vb, tri_q, hbias, batch, seq, tq)
            ms = _sb_sample(q_rows, _new_keys_t(k[n_p:], db, tokens), _new_keys_t(v[n_p:], db, tokens),
                            sb_kt, sb_vt, page_table, tri_p, hbias, j, pages)
            sb_kp.append(kp)
            sb_vp.append(vp)
            sb_ks.append(ks)
            sb_vs.append(vs)
        ms = _from_rows(ms, db, tokens)
        x = _oproj_ln(mp, ms, x, bf(w_o[i]), row(ln_gain[i, 1]), row(ln_bias[i, 1]), alpha, tm)
        x = _ffn_ln(x, bf(w_ffn2_gate[i]), bf(w_ffn2_up[i]), bf(w_ffn2_down[i]),
                    row(ln_gain[i, 2]), row(ln_bias[i, 2]), alpha, tm, tf)

    return (x[:n_p].reshape(batch, seq, d), x[n_p:].reshape(db, tokens, d),
            jnp.stack(swa_kp), jnp.stack(swa_vp), jnp.stack(sb_kp), jnp.stack(sb_vp),
            jnp.stack(swa_ks), jnp.stack(swa_vs), jnp.stack(sb_ks), jnp.stack(sb_vs))
```

```python
import functools
import math

import jax
import jax.numpy as jnp
from jax import lax
from jax.experimental import pallas as pl
from jax.experimental.pallas import tpu as pltpu

D_MODEL = 1024
N_HEADS = 16
HEAD_DIM = 64
N_KV_HEADS = 4
GROUP = N_HEADS // N_KV_HEADS
KV_WIDTH = N_KV_HEADS * HEAD_DIM
WINDOW = 128
BLOCK = 128
PAGE_SIZE = 128
NUM_BUCKETS = 32
MAX_DISTANCE = 128
LN_EPS = 1e-5
SCALE = HEAD_DIM ** -0.5
LOG2E = math.log2(math.e)
NEG = -1e30

BF16 = jnp.bfloat16
F32 = jnp.float32

VMEM_LIMIT = 56 * 1024 * 1024


def _params(sem):
    return pltpu.CompilerParams(dimension_semantics=sem, vmem_limit_bytes=VMEM_LIMIT)


def _dot(a, b):
    return jnp.dot(a, b, preferred_element_type=F32)


def _dot_nt(a, b):
    return lax.dot_general(a, b, (((1,), (1,)), ((), ())), preferred_element_type=F32)


def _layer_norm(y, g, b):
    mu = jnp.mean(y, axis=-1, keepdims=True)
    d = y - mu
    var = jnp.mean(d * d, axis=-1, keepdims=True)
    return d * lax.rsqrt(var + LN_EPS) * g + b


def _ffn_kernel(x_ref, wg_ref, wu_ref, wd_ref, g_ref, b_ref, o_ref, xb_ref, acc_ref, *, alpha):
    j = pl.program_id(1)

    @pl.when(j == 0)
    def _():
        xb_ref[...] = x_ref[...].astype(BF16)
        acc_ref[...] = jnp.zeros_like(acc_ref)

    xb = xb_ref[...]
    gate = _dot(xb, wg_ref[...])
    up = _dot(xb, wu_ref[...])
    h = gate * jax.nn.sigmoid(gate) * up
    acc_ref[...] += _dot(h.astype(BF16), wd_ref[...])

    @pl.when(j == pl.num_programs(1) - 1)
    def _():
        y = alpha * x_ref[...] + 0.5 * acc_ref[...]
        o_ref[...] = _layer_norm(y, g_ref[...], b_ref[...])


def _ffn_ln(x, wg, wu, wd, gain, bias, alpha, tm, tf):
    n, d = x.shape
    dff = wg.shape[1]
    return pl.pallas_call(
        functools.partial(_ffn_kernel, alpha=alpha),
        grid=(n // tm, dff // tf),
        in_specs=[
            pl.BlockSpec((tm, d), lambda i, j: (i, 0)),
            pl.BlockSpec((d, tf), lambda i, j: (0, j)),
            pl.BlockSpec((d, tf), lambda i, j: (0, j)),
            pl.BlockSpec((tf, d), lambda i, j: (j, 0)),
            pl.BlockSpec((1, d), lambda i, j: (0, 0)),
            pl.BlockSpec((1, d), lambda i, j: (0, 0)),
        ],
        out_specs=pl.BlockSpec((tm, d), lambda i, j: (i, 0)),
        out_shape=jax.ShapeDtypeStruct((n, d), F32),
        scratch_shapes=[pltpu.VMEM((tm, d), BF16), pltpu.VMEM((tm, d), F32)],
        compiler_params=_params(("parallel", "arbitrary")),
        name="ffn_ln",
    )(x, wg, wu, wd, gain, bias)


def _qkv_kernel(x_ref, w_ref, q_ref, k_ref, v_ref, kb_ref, vb_ref, *, q_scale):
    nq = N_HEADS * HEAD_DIM
    qkv = _dot(x_ref[...].astype(BF16), w_ref[...])
    q_ref[...] = (qkv[:, :nq] * q_scale).astype(BF16)
    k = qkv[:, nq:nq + KV_WIDTH]
    v = qkv[:, nq + KV_WIDTH:]
    k_ref[...] = k
    v_ref[...] = v
    kb_ref[...] = k.astype(BF16)
    vb_ref[...] = v.astype(BF16)


def _qkv(x, w, tm, q_scale):
    n, d = x.shape
    width = w.shape[1]
    nq = N_HEADS * HEAD_DIM
    row = lambda i: (i, 0)
    return pl.pallas_call(
        functools.partial(_qkv_kernel, q_scale=q_scale),
        grid=(n // tm,),
        in_specs=[pl.BlockSpec((tm, d), row), pl.BlockSpec((d, width), lambda i: (0, 0))],
        out_specs=[
            pl.BlockSpec((tm, nq), row),
            pl.BlockSpec((tm, KV_WIDTH), row),
            pl.BlockSpec((tm, KV_WIDTH), row),
            pl.BlockSpec((tm, KV_WIDTH), row),
            pl.BlockSpec((tm, KV_WIDTH), row),
        ],
        out_shape=[
            jax.ShapeDtypeStruct((n, nq), BF16),
            jax.ShapeDtypeStruct((n, KV_WIDTH), F32),
            jax.ShapeDtypeStruct((n, KV_WIDTH), F32),
            jax.ShapeDtypeStruct((n, KV_WIDTH), BF16),
            jax.ShapeDtypeStruct((n, KV_WIDTH), BF16),
        ],
        compiler_params=_params(("parallel",)),
        name="qkv_proj",
    )(x, w)


def _oproj_kernel(mp_ref, ms_ref, x_ref, w_ref, g_ref, b_ref, o_ref, *, alpha, n_prompt_tiles):
    i = pl.program_id(0)

    def finish(m):
        y = alpha * x_ref[...] + _dot(m, w_ref[...])
        o_ref[...] = _layer_norm(y, g_ref[...], b_ref[...])

    @pl.when(i < n_prompt_tiles)
    def _():
        finish(mp_ref[...])

    @pl.when(i >= n_prompt_tiles)
    def _():
        finish(ms_ref[...])


def _oproj_ln(mp, ms, x, w, gain, bias, alpha, tm):
    n, d = x.shape
    npt = mp.shape[0] // tm
    nst = ms.shape[0] // tm
    return pl.pallas_call(
        functools.partial(_oproj_kernel, alpha=alpha, n_prompt_tiles=npt),
        grid=(n // tm,),
        in_specs=[
            pl.BlockSpec((tm, d), lambda i: (jnp.minimum(i, npt - 1), 0)),
            pl.BlockSpec((tm, d), lambda i: (jnp.clip(i - npt, 0, nst - 1), 0)),
            pl.BlockSpec((tm, d), lambda i: (i, 0)),
            pl.BlockSpec((d, d), lambda i: (0, 0)),
            pl.BlockSpec((1, d), lambda i: (0, 0)),
            pl.BlockSpec((1, d), lambda i: (0, 0)),
        ],
        out_specs=pl.BlockSpec((tm, d), lambda i: (i, 0)),
        out_shape=jax.ShapeDtypeStruct((n, d), F32),
        compiler_params=_params(("parallel",)),
        name="oproj_ln",
    )(mp, ms, x, w, gain, bias)


def _swa_prompt_kernel(sink_ref, q_ref, kp_ref, kc_ref, vp_ref, vc_ref, bp_ref, bc_ref, o_ref):
    n = pl.program_id(1)
    m_rows = GROUP * BLOCK
    for kvh in range(N_KV_HEADS):
        cs = slice(kvh * HEAD_DIM, (kvh + 1) * HEAD_DIM)
        q = jnp.concatenate(
            [q_ref[:, (kvh * GROUP + g) * HEAD_DIM:(kvh * GROUP + g + 1) * HEAD_DIM] for g in range(GROUP)],
            axis=0)
        sink = jnp.concatenate(
            [jnp.full((BLOCK, 1), sink_ref[kvh * GROUP + g], F32) for g in range(GROUP)], axis=0)
        sc = _dot_nt(q, kc_ref[:, cs]) + bc_ref[kvh]
        sp = _dot_nt(q, kp_ref[:, cs]) + bp_ref[kvh]
        sp = jnp.where(n > 0, sp, NEG)
        m = jnp.maximum(jnp.maximum(jnp.max(sc, axis=-1, keepdims=True),
                                    jnp.max(sp, axis=-1, keepdims=True)), sink)
        ec = jnp.exp(sc - m)
        ep = jnp.exp(sp - m)
        den = (jnp.sum(ec, axis=-1, keepdims=True) + jnp.sum(ep, axis=-1, keepdims=True)
               + jnp.exp(sink - m))
        o = (_dot(ec.astype(BF16), vc_ref[:, cs]) + _dot(ep.astype(BF16), vp_ref[:, cs])) / den
        for g in range(GROUP):
            h = kvh * GROUP + g
            o_ref[:, h * HEAD_DIM:(h + 1) * HEAD_DIM] = o[g * BLOCK:(g + 1) * BLOCK].astype(BF16)
    del m_rows


def _swa_prompt(q, kb, vb, bias_prev, bias_cur, sinks, batch, seq):
    nb = seq // BLOCK
    nq = N_HEADS * HEAD_DIM
    cur = lambda b, n: (b * nb + n, 0)
    prev = lambda b, n: (b * nb + jnp.maximum(n - 1, 0), 0)
    full3 = lambda b, n: (0, 0, 0)
    return pl.pallas_call(
        _swa_prompt_kernel,
        grid=(batch, nb),
        in_specs=[
            pl.BlockSpec(memory_space=pltpu.SMEM),
            pl.BlockSpec((BLOCK, nq), cur),
            pl.BlockSpec((BLOCK, KV_WIDTH), prev),
            pl.BlockSpec((BLOCK, KV_WIDTH), cur),
            pl.BlockSpec((BLOCK, KV_WIDTH), prev),
            pl.BlockSpec((BLOCK, KV_WIDTH), cur),
            pl.BlockSpec((N_KV_HEADS, GROUP * BLOCK, BLOCK), full3),
            pl.BlockSpec((N_KV_HEADS, GROUP * BLOCK, BLOCK), full3),
        ],
        out_specs=pl.BlockSpec((BLOCK, nq), cur),
        out_shape=jax.ShapeDtypeStruct((batch * seq, nq), BF16),
        compiler_params=_params(("parallel", "arbitrary")),
        name="swa_prompt",
    )(sinks, q, kb, kb, vb, vb, bias_prev, bias_cur)


def _row_kv_mask():
    rows = N_HEADS * 8
    r = lax.broadcasted_iota(jnp.int32, (rows, KV_WIDTH), 0)
    c = lax.broadcasted_iota(jnp.int32, (rows, KV_WIDTH), 1)
    return (r >> 5) == (c >> 6)


def _block_diag_q(q_rows, mask):
    q4 = jnp.concatenate([q_rows] * N_KV_HEADS, axis=-1)
    return jnp.where(mask, q4, jnp.zeros_like(q4))


def _pick_kv_block(acc, mask):
    z = jnp.where(mask, acc, 0.0)
    out = z[:, 0:HEAD_DIM]
    for kvh in range(1, N_KV_HEADS):
        out = out + z[:, kvh * HEAD_DIM:(kvh + 1) * HEAD_DIM]
    return out


def _swa_sample_kernel(sink_ref, q_ref, kt_ref, vt_ref, kn_ref, vn_ref, bb_ref, bn_ref, o_ref, *, seqs):
    mask = _row_kv_mask()
    rows = N_HEADS * 8
    sink = jnp.concatenate([jnp.full((8, 1), sink_ref[h], F32) for h in range(N_HEADS)], axis=0)
    for s in range(seqs):
        qbd = _block_diag_q(q_ref[s], mask)
        sb = _dot(qbd, kt_ref[s].astype(BF16)) + bb_ref[...]
        sn = _dot_nt(qbd, kn_ref[s]) + bn_ref[...]
        m = jnp.maximum(jnp.maximum(jnp.max(sb, axis=-1, keepdims=True),
                                    jnp.max(sn, axis=-1, keepdims=True)), sink)
        eb = jnp.exp(sb - m)
        en = jnp.exp(sn - m)
        den = (jnp.sum(eb, axis=-1, keepdims=True) + jnp.sum(en, axis=-1, keepdims=True)
               + jnp.exp(sink - m))
        acc = _dot_nt(eb.astype(BF16), vt_ref[s].astype(BF16)) + _dot(en.astype(BF16), vn_ref[s])
        o_ref[s] = (_pick_kv_block(acc, mask) / den).astype(BF16)
    del rows


def _swa_sample(q_rows, kt, vt, kn, vn, bias_buf, bias_new, sinks, layer, seqs):
    db = q_rows.shape[0]
    rows = N_HEADS * 8
    seq3 = lambda i: (i, 0, 0)
    cache = lambda i: (layer, i, 0, 0)
    return pl.pallas_call(
        functools.partial(_swa_sample_kernel, seqs=seqs),
        grid=(db // seqs,),
        in_specs=[
            pl.BlockSpec(memory_space=pltpu.SMEM),
            pl.BlockSpec((seqs, rows, HEAD_DIM), seq3),
            pl.BlockSpec((None, seqs, KV_WIDTH, WINDOW), cache),
            pl.BlockSpec((None, seqs, KV_WIDTH, WINDOW), cache),
            pl.BlockSpec((seqs, WINDOW, KV_WIDTH), seq3),
            pl.BlockSpec((seqs, WINDOW, KV_WIDTH), seq3),
            pl.BlockSpec((rows, WINDOW), lambda i: (0, 0)),
            pl.BlockSpec((rows, WINDOW), lambda i: (0, 0)),
        ],
        out_specs=pl.BlockSpec((seqs, rows, HEAD_DIM), seq3),
        out_shape=jax.ShapeDtypeStruct((db, rows, HEAD_DIM), BF16),
        compiler_params=_params(("parallel",)),
        name="swa_sample",
    )(sinks, q_rows, kt, vt, kn, vn, bias_buf, bias_new)


def _sb_drop(z2, valid):
    neg_abs = lax.bitcast_convert_type(
        lax.bitcast_convert_type(z2, jnp.uint32) | jnp.uint32(0x80000000), F32)
    drop = jnp.maximum(z2, 0.0) + jnp.log2(1.0 + jnp.exp2(neg_abs))
    if valid is not None:
        drop = jnp.where(valid, drop, 0.0)
    return drop


def _sb_weights(z2, drop, later, carry, valid):
    w = jnp.exp2(z2 - drop - later - carry)
    if valid is not None:
        w = jnp.where(valid, w, 0.0)
    return w


def _half_lanes(x, half):
    lane = lax.broadcasted_iota(jnp.int32, x.shape, 1)
    return jnp.where((lane >> 6) == half, x, jnp.zeros_like(x))


def _sb_prompt_kernel(hb_ref, q_ref, k_ref, v_ref, tri_ref, o_ref, acc_ref, cs_ref, *, tq):
    i = pl.program_id(1)
    m_rows = GROUP * tq
    tri = tri_ref[...]
    t_in = lax.broadcasted_iota(jnp.int32, (m_rows, tq), 0) & (tq - 1)
    s_in = lax.broadcasted_iota(jnp.int32, (m_rows, tq), 1)
    diag_valid = s_in < t_in
    for kvh in range(N_KV_HEADS):
        half = kvh % 2
        lanes = slice((kvh // 2) * 128, (kvh // 2 + 1) * 128)
        q_parts = []
        for g in range(GROUP):
            h = kvh * GROUP + g
            qh = q_ref[:, (h // 2) * 128:(h // 2 + 1) * 128].astype(F32)
            if h % 2 != half:
                qh = pltpu.roll(qh, 64, axis=1)
            q_parts.append(_half_lanes(qh, half).astype(BF16))
        q = jnp.concatenate(q_parts, axis=0)
        hb = jnp.concatenate(
            [jnp.full((tq, 1), hb_ref[kvh * GROUP + g], F32) for g in range(GROUP)], axis=0)

        def block(j, nsub, valid):
            rows = pl.ds(pl.multiple_of(j * tq, tq), nsub * tq)
            z2 = _dot_nt(q, k_ref[rows, lanes]) + hb
            drop = _sb_drop(z2, valid)
            drop_b = drop.astype(BF16)
            carry = cs_ref[...]
            ws = [None] * nsub
            for p in reversed(range(nsub)):
                cols = slice(p * tq, (p + 1) * tq)
                later = _dot(drop_b[:, cols], tri)
                ws[p] = _sb_weights(z2[:, cols], drop[:, cols], later, carry, valid)
                carry = carry + later[:, 0:1] + drop[:, p * tq:p * tq + 1]
            w = ws[0] if nsub == 1 else jnp.concatenate(ws, axis=1)
            acc_ref[...] += _dot(w.astype(BF16), v_ref[rows, lanes])
            cs_ref[...] = carry

        acc_ref[...] = jnp.zeros_like(acc_ref)
        cs_ref[...] = jnp.zeros_like(cs_ref)
        block(i, 1, diag_valid)

        @pl.when(i % 2 == 1)
        def _():
            block(i - 1, 1, None)

        def body(jj, c):
            block(2 * (i // 2 - 1 - jj), 2, None)
            return c

        lax.fori_loop(0, i // 2, body, 0)
        out = acc_ref[...]
        for g2 in range(GROUP // 2):
            even = out[(2 * g2) * tq:(2 * g2 + 1) * tq]
            odd = out[(2 * g2 + 1) * tq:(2 * g2 + 2) * tq]
            if half == 0:
                odd = pltpu.roll(odd, 64, axis=1)
            else:
                even = pltpu.roll(even, 64, axis=1)
            slot = kvh * (GROUP // 2) + g2
            lane = lax.broadcasted_iota(jnp.int32, even.shape, 1)
            o_ref[:, slot * 128:(slot + 1) * 128] = jnp.where(lane < HEAD_DIM, even, odd).astype(BF16)


def _sb_prompt(q, kb, vb, tri, hbias, batch, seq, tq):
    nq_blocks = seq // tq
    nq = N_HEADS * HEAD_DIM
    return pl.pallas_call(
        functools.partial(_sb_prompt_kernel, tq=tq),
        grid=(batch, nq_blocks),
        in_specs=[
            pl.BlockSpec(memory_space=pltpu.SMEM),
            pl.BlockSpec((tq, nq), lambda b, i: (b * nq_blocks + i, 0)),
            pl.BlockSpec((seq, KV_WIDTH), lambda b, i: (b, 0)),
            pl.BlockSpec((seq, KV_WIDTH), lambda b, i: (b, 0)),
            pl.BlockSpec((tq, tq), lambda b, i: (0, 0)),
        ],
        out_specs=pl.BlockSpec((tq, nq), lambda b, i: (b * nq_blocks + i, 0)),
        out_shape=jax.ShapeDtypeStruct((batch * seq, nq), BF16),
        scratch_shapes=[pltpu.VMEM((GROUP * tq, 2 * HEAD_DIM), F32), pltpu.VMEM((GROUP * tq, 1), F32)],
        compiler_params=_params(("parallel", "arbitrary")),
        name="sb_prompt",
    )(hbias, q, kb, vb, tri)


def _sb_sample_kernel(pt_ref, hb_ref, q_ref, kn_ref, vn_ref, tri_ref, *rest, pages):
    del pt_ref
    k_refs = rest[:pages]
    v_refs = rest[pages:2 * pages]
    o_ref, acc_ref, cs_ref, qbd_ref = rest[2 * pages:]
    c = pl.program_id(1)
    rows = N_HEADS * 8
    mask = _row_kv_mask()
    tri = tri_ref[...]
    hb = jnp.concatenate([jnp.full((8, 1), hb_ref[h], F32) for h in range(N_HEADS)], axis=0)

    def chunk(kts, vts, valid):
        n = len(kts)
        z2 = _dot(qbd_ref[...], jnp.concatenate(kts, axis=1)) + hb
        drop = _sb_drop(z2, valid)
        drop_b = drop.astype(BF16)
        carry = cs_ref[...]
        ws = []
        for p in range(n):
            cols = slice(p * PAGE_SIZE, (p + 1) * PAGE_SIZE)
            later = _dot(drop_b[:, cols], tri)
            ws.append(_sb_weights(z2[:, cols], drop[:, cols], later, carry,
                                  None if valid is None else valid[:, cols]))
            carry = carry + later[:, 0:1] + drop[:, p * PAGE_SIZE:p * PAGE_SIZE + 1]
        w = jnp.concatenate(ws, axis=1).astype(BF16)
        acc_ref[...] += _dot_nt(w, jnp.concatenate(vts, axis=1))
        cs_ref[...] = carry

    @pl.when(c == 0)
    def _():
        qbd_ref[...] = _block_diag_q(q_ref[...], mask)
        acc_ref[...] = jnp.zeros_like(acc_ref)
        cs_ref[...] = jnp.zeros_like(cs_ref)
        t_in = lax.broadcasted_iota(jnp.int32, (rows, PAGE_SIZE), 0) & 7
        s_in = lax.broadcasted_iota(jnp.int32, (rows, PAGE_SIZE), 1)
        chunk([kn_ref[...]], [vn_ref[...]], s_in < t_in)

    chunk([k_refs[p][...].astype(BF16) for p in range(pages)],
          [v_refs[p][...].astype(BF16) for p in range(pages)], None)

    @pl.when(c == pl.num_programs(1) - 1)
    def _():
        o_ref[...] = _pick_kv_block(acc_ref[...], mask).astype(BF16)


def _sb_sample(q_rows, knt, vnt, pool_kt, pool_vt, page_table, tri, hbias, layer, pages):
    db, n_pages = page_table.shape
    rows = N_HEADS * 8
    n_chunks = n_pages // pages

    def page_map(p):
        return lambda b, c, pt: (layer, pt[b, n_pages - 1 - (c * pages + p)], 0, 0)

    seq3 = lambda b, c, pt: (b, 0, 0)
    page_spec = [pl.BlockSpec((None, None, KV_WIDTH, PAGE_SIZE), page_map(p)) for p in range(pages)]
    grid_spec = pltpu.PrefetchScalarGridSpec(
        num_scalar_prefetch=1,
        grid=(db, n_chunks),
        in_specs=[
            pl.BlockSpec(memory_space=pltpu.SMEM),
            pl.BlockSpec((None, rows, HEAD_DIM), seq3),
            pl.BlockSpec((None, KV_WIDTH, PAGE_SIZE), seq3),
            pl.BlockSpec((None, KV_WIDTH, PAGE_SIZE), seq3),
            pl.BlockSpec((PAGE_SIZE, PAGE_SIZE), lambda b, c, pt: (0, 0)),
        ] + page_spec + page_spec,
        out_specs=pl.BlockSpec((None, rows, HEAD_DIM), seq3),
        scratch_shapes=[
            pltpu.VMEM((rows, KV_WIDTH), F32),
            pltpu.VMEM((rows, 1), F32),
            pltpu.VMEM((rows, KV_WIDTH), BF16),
        ],
    )
    return pl.pallas_call(
        functools.partial(_sb_sample_kernel, pages=pages),
        grid_spec=grid_spec,
        out_shape=jax.ShapeDtypeStruct((db, rows, HEAD_DIM), BF16),
        compiler_params=_params(("parallel", "arbitrary")),
        name="sb_sample",
    )(page_table, hbias, q_rows, knt, vnt, tri, *([pool_kt] * pages), *([pool_vt] * pages))


def _rel_bucket(dist):
    n = jnp.clip(dist, 0, MAX_DISTANCE)
    max_exact = NUM_BUCKETS // 2
    nf = jnp.maximum(n, 1).astype(F32)
    large = max_exact + (jnp.log(nf / max_exact) / math.log(MAX_DISTANCE / max_exact)
                         * (NUM_BUCKETS - max_exact)).astype(jnp.int32)
    large = jnp.minimum(large, NUM_BUCKETS - 1)
    return jnp.where(n < max_exact, n, large)


def _masked_bias(dist, rel_bias):
    per_dist = rel_bias.astype(F32)[_rel_bucket(jnp.arange(MAX_DISTANCE + 1))]
    onehot = jnp.clip(dist, 0, MAX_DISTANCE)[..., None] == jnp.arange(MAX_DISTANCE + 1)
    b = jnp.einsum('qsd,dh->hqs', onehot.astype(F32), per_dist, precision=lax.Precision.HIGHEST)
    return jnp.where(((dist >= 0) & (dist <= WINDOW))[None], b, NEG)


def _tri(n):
    j = lax.broadcasted_iota(jnp.int32, (n, n), 0)
    s = lax.broadcasted_iota(jnp.int32, (n, n), 1)
    return (j > s).astype(BF16)


def _tables(rel_bias, tokens, tq):
    q_i = jnp.arange(BLOCK)[:, None]
    s_i = jnp.arange(BLOCK)[None, :]
    kv_rows = lambda b: b.reshape(N_KV_HEADS, GROUP * b.shape[1], b.shape[2])
    t_i = jnp.arange(tokens)[:, None]
    w_i = jnp.arange(WINDOW)[None, :]
    return {
        "bias_prev": kv_rows(_masked_bias(q_i + BLOCK - s_i, rel_bias)),
        "bias_cur": kv_rows(_masked_bias(q_i - s_i, rel_bias)),
        "bias_buf": _masked_bias(t_i + WINDOW - w_i, rel_bias).reshape(N_HEADS * tokens, WINDOW),
        "bias_new": jnp.where(w_i < tokens, _masked_bias(t_i - w_i, rel_bias), NEG).reshape(
            N_HEADS * tokens, WINDOW),
        "tri_q": _tri(tq),
        "tri_p": _tri(PAGE_SIZE),
    }


def _to_rows(t, db, tokens, width_heads):
    t = t.reshape(db, tokens, width_heads, HEAD_DIM)
    return jnp.transpose(t, (0, 2, 1, 3)).reshape(db, width_heads * tokens, HEAD_DIM)


def _from_rows(t, db, tokens):
    t = t.reshape(db, N_HEADS, tokens, HEAD_DIM)
    return jnp.transpose(t, (0, 2, 1, 3)).reshape(db * tokens, N_HEADS * HEAD_DIM)


def _new_keys_t(t, db, tokens):
    t = jnp.transpose(t.reshape(db, tokens, KV_WIDTH), (0, 2, 1)).astype(BF16)
    return jnp.pad(t, ((0, 0), (0, 0), (0, PAGE_SIZE - tokens)))


def _new_keys(t, db, tokens):
    t = t.reshape(db, tokens, KV_WIDTH).astype(BF16)
    return jnp.pad(t, ((0, 0), (0, WINDOW - tokens), (0, 0)))


def _cache_t(cache):
    l, x, r = cache.shape[:3]
    return jnp.transpose(cache, (0, 1, 3, 4, 2)).reshape(l, x, KV_WIDTH, r)


def kernel(x_prompt, x_sample, cache_swa_k, cache_swa_v, cache_sb_k, cache_sb_v, page_table, rel_bias,
           attn_sinks, sb_logit_bias, w_qkv, w_o, w_ffn1_gate, w_ffn1_up, w_ffn1_down, w_ffn2_gate,
           w_ffn2_up, w_ffn2_down, ln_gain, ln_bias):
    batch, seq, d = x_prompt.shape
    db, tokens, _ = x_sample.shape
    depth = w_qkv.shape[0]
    dff = w_ffn1_gate.shape[2]
    n_p = batch * seq
    n_s = db * tokens
    assert tokens == 8 and d == D_MODEL
    alpha = (2.0 * depth) ** 0.25

    tm = math.gcd(1024, math.gcd(n_p, n_s))
    tf = 256 if dff % 256 == 0 else 128
    tq = min(256, seq)
    pages = math.gcd(16, page_table.shape[1])
    seqs = math.gcd(8, db)

    x = jnp.concatenate([x_prompt.reshape(n_p, d), x_sample.reshape(n_s, d)], axis=0)

    tabs = _tables(rel_bias, tokens, tq)
    bias_prev, bias_cur, bias_buf, bias_new = (tabs[n] for n in ("bias_prev", "bias_cur", "bias_buf", "bias_new"))
    tri_q, tri_p = tabs["tri_q"], tabs["tri_p"]

    swa_kt, swa_vt = _cache_t(cache_swa_k), _cache_t(cache_swa_v)
    sb_kt, sb_vt = _cache_t(cache_sb_k), _cache_t(cache_sb_v)

    bf = lambda w: w.astype(BF16)
    row = lambda v: v.reshape(1, d)

    swa_kp, swa_vp, swa_ks, swa_vs = [], [], [], []
    sb_kp, sb_vp, sb_ks, sb_vs = [], [], [], []
    for i in range(depth):
        x = _ffn_ln(x, bf(w_ffn1_gate[i]), bf(w_ffn1_up[i]), bf(w_ffn1_down[i]),
                    row(ln_gain[i, 0]), row(ln_bias[i, 0]), alpha, tm, tf)
        q_scale = SCALE if i % 2 == 0 else SCALE * LOG2E
        q, k, v, kb, vb = _qkv(x, bf(w_qkv[i]), tm, q_scale)
        kp = k[:n_p].reshape(batch, seq, N_KV_HEADS, HEAD_DIM)
        vp = v[:n_p].reshape(batch, seq, N_KV_HEADS, HEAD_DIM)
        ks = k[n_p:].reshape(db, tokens, N_KV_HEADS, HEAD_DIM)
        vs = v[n_p:].reshape(db, tokens, N_KV_HEADS, HEAD_DIM)
        q_rows = _to_rows(q[n_p:], db, tokens, N_HEADS)
        j = i // 2
        if i % 2 == 0:
            sinks = attn_sinks[j].astype(F32)
            mp = _swa_prompt(q, kb, vb, bias_prev, bias_cur, sinks, batch, seq)
            ms = _swa_sample(q_rows, swa_kt, swa_vt, _new_keys(k[n_p:], db, tokens),
                             _new_keys(v[n_p:], db, tokens), bias_buf, bias_new, sinks, j, seqs)
            swa_kp.append(kp[:, -WINDOW:])
            swa_vp.append(vp[:, -WINDOW:])
            swa_ks.append(jnp.concatenate([cache_swa_k[j], ks], axis=1)[:, tokens:])
            swa_vs.append(jnp.concatenate([cache_swa_v[j], vs], axis=1)[:, tokens:])
        else:
            hbias = sb_logit_bias[j].astype(F32) * LOG2E
            mp = _sb_prompt(q, kb, vb, tri_q, hbias, batch, seq, tq)
            ms = _sb_sample(q_rows, _new_keys_t(k[n_p:], db, tokens), _new_keys_t(v[n_p:], db, tokens),
                            sb_kt, sb_vt, page_table, tri_p, hbias, j, pages)
            sb_kp.append(kp)
            sb_vp.append(vp)
            sb_ks.append(ks)
            sb_vs.append(vs)
        ms = _from_rows(ms, db, tokens)
        x = _oproj_ln(mp, ms, x, bf(w_o[i]), row(ln_gain[i, 1]), row(ln_bias[i, 1]), alpha, tm)
        x = _ffn_ln(x, bf(w_ffn2_gate[i]), bf(w_ffn2_up[i]), bf(w_ffn2_down[i]),
                    row(ln_gain[i, 2]), row(ln_bias[i, 2]), alpha, tm, tf)

    return (x[:n_p].reshape(batch, seq, d), x[n_p:].reshape(db, tokens, d),
            jnp.stack(swa_kp), jnp.stack(swa_vp), jnp.stack(sb_kp), jnp.stack(sb_vp),
            jnp.stack(swa_ks), jnp.stack(swa_vs), jnp.stack(sb_ks), jnp.stack(sb_vs))
```
